```python
import math
import jax, jax.numpy as jnp
from jax import lax
import numpy as np

D_MODEL = 1024
BATCH = 8
SEQ = 4096
DEPTH = 1

CHUNK = 64
N_META = 16
N_PAD = CHUNK - N_META
EPS = 1e-6

SSD_EXPAND = 2
SSD_D_INNER = SSD_EXPAND * D_MODEL
SSD_HEAD_DIM = 64
SSD_N_HEADS = SSD_D_INNER // SSD_HEAD_DIM
SSD_N_GROUPS = 8
SSD_HEADS_PER_GROUP = SSD_N_HEADS // SSD_N_GROUPS
SSD_D_STATE = 128
SSD_CONV_W = 4
SSD_CONV_DIM = SSD_D_INNER + 2 * SSD_N_GROUPS * SSD_D_STATE

SC_WIDTH = D_MODEL
SC_CONV_W = 3

N_BRANCH = 2

PEER_HEADS = 8
PEER_N_KEYS = 128
PEER_N_EXPERTS = PEER_N_KEYS * PEER_N_KEYS
PEER_D_KEY = 256
PEER_HALF = PEER_D_KEY // 2
PEER_TOPK = 16
PEER_BLOCK = CHUNK

PROJ_SIZES = (
    SSD_D_INNER,
    SSD_D_INNER,
    SSD_N_GROUPS * SSD_D_STATE,
    SSD_N_GROUPS * SSD_D_STATE,
    SSD_N_HEADS,
    SC_WIDTH,
    SC_WIDTH,
    SC_WIDTH,
    N_BRANCH * D_MODEL,
)
PROJ_TOTAL = sum(PROJ_SIZES)

kernel_name = "hybrid_ssd_shortconv_peer_block"


def rmsnorm(x, g):
    xf = x.astype(jnp.float32)
    y = xf * lax.rsqrt(jnp.mean(xf * xf, axis=-1, keepdims=True) + EPS)
    return (y * g).astype(x.dtype)


def group_rmsnorm(y, g, groups):
    shp = y.shape
    yg = y.reshape(shp[:-1] + (groups, shp[-1] // groups)).astype(jnp.float32)
    yg = yg * lax.rsqrt(jnp.mean(yg * yg, axis=-1, keepdims=True) + EPS)
    return yg.reshape(shp) * g


def causal_dwconv(x, w):
    K = w.shape[-1]
    L = x.shape[1]
    xp = jnp.pad(x, ((0, 0), (K - 1, 0), (0, 0)))
    y = xp[:, 0:L, :] * w[:, 0]
    for k in range(1, K):
        y = y + xp[:, k:k + L, :] * w[:, k]
    return y


def ssd_chunked_scan(xdt, adt, bm, cm):
    b, L = xdt.shape[:2]
    nc = L // CHUNK

    def to_chunks(t):
        return jnp.moveaxis(t.reshape((b, nc, CHUNK) + t.shape[2:]), 1, 0)

    causal = jnp.tril(jnp.ones((CHUNK, CHUNK), dtype=bool))[None, :, :, None, None]

    def step(state, inp):
        xc, ac, bc, cc = inp
        acum = jnp.cumsum(ac.astype(jnp.float32), axis=1)
        seg = acum[:, :, None] - acum[:, None, :]
        decay = jnp.exp(jnp.where(causal, seg, -jnp.inf))
        cb = jnp.einsum('blgn,bsgn->blsg', cc, bc).astype(jnp.float32)
        y_diag = jnp.einsum('blsgr,bsgrp->blgrp', cb[..., None] * decay, xc)
        y_off = jnp.einsum('blgn,bgrpn->blgrp', cc, state) * jnp.exp(acum)[..., None]
        to_end = jnp.exp(acum[:, -1:] - acum)
        new_state = (state * jnp.exp(acum[:, -1])[..., None, None]
                     + jnp.einsum('blgn,blgr,blgrp->bgrpn', bc, to_end, xc))
        return new_state, y_diag + y_off

    state0 = jnp.zeros((b, SSD_N_GROUPS, SSD_HEADS_PER_GROUP, SSD_HEAD_DIM, SSD_D_STATE), jnp.float32)
    _, y = lax.scan(step, state0, (to_chunks(xdt), to_chunks(adt), to_chunks(bm), to_chunks(cm)))
    return jnp.moveaxis(y, 0, 1).reshape((b, L) + xdt.shape[2:])


def ssd_branch(z, xs, bs, cs, dt, valid, conv_w, conv_b, dt_bias, a_log, d_skip, norm_g, w_out):
    b, L, _ = xs.shape
    xbc = jnp.concatenate([xs, bs, cs], axis=-1)
    xbc = jax.nn.silu(causal_dwconv(xbc, conv_w) + conv_b) * valid
    xs = xbc[..., :SSD_D_INNER]
    bs = xbc[..., SSD_D_INNER:SSD_D_INNER + SSD_N_GROUPS * SSD_D_STATE]
    cs = xbc[..., SSD_D_INNER + SSD_N_GROUPS * SSD_D_STATE:]
    dt = jax.nn.softplus(dt.astype(jnp.float32) + dt_bias)
    a = -jnp.exp(a_log.astype(jnp.float32))
    xh = xs.reshape(b, L, SSD_N_GROUPS, SSD_HEADS_PER_GROUP, SSD_HEAD_DIM)
    dtg = dt.reshape(b, L, SSD_N_GROUPS, SSD_HEADS_PER_GROUP)
    y = ssd_chunked_scan(xh * dtg[..., None],
                         dtg * a.reshape(SSD_N_GROUPS, SSD_HEADS_PER_GROUP),
                         bs.reshape(b, L, SSD_N_GROUPS, SSD_D_STATE),
                         cs.reshape(b, L, SSD_N_GROUPS, SSD_D_STATE))
    y = y + xh * d_skip.reshape(SSD_N_GROUPS, SSD_HEADS_PER_GROUP)[..., None]
    y = y.reshape(b, L, SSD_D_INNER)
    y = group_rmsnorm(y * jax.nn.silu(z.astype(jnp.float32)), norm_g, SSD_N_GROUPS)
    return y.astype(xs.dtype) @ w_out


def shortconv_branch(sb, sc, sx, conv_w, w_out):
    return (sb * causal_dwconv(sc * sx, conv_w)) @ w_out


def peer_ffn(h, w_q, sub_keys, expert_u, expert_v):
    T, D = h.shape
    blocks = h.reshape(T // PEER_BLOCK, PEER_BLOCK, D)

    def one(xb):
        q = (xb @ w_q).reshape(PEER_BLOCK, PEER_HEADS, 2, PEER_HALF)
        s = jnp.einsum('thjd,hjkd->thjk', q, sub_keys).astype(jnp.float32)
        sv, si = lax.top_k(s, PEER_TOPK)
        cand = sv[:, :, 0, :, None] + sv[:, :, 1, None, :]
        cs, ci = lax.top_k(cand.reshape(PEER_BLOCK, PEER_HEADS, PEER_TOPK * PEER_TOPK), PEER_TOPK)
        i1 = jnp.take_along_axis(si[:, :, 0], ci // PEER_TOPK, axis=-1)
        i2 = jnp.take_along_axis(si[:, :, 1], ci % PEER_TOPK, axis=-1)
        eidx = i1 * PEER_N_KEYS + i2
        g = jax.nn.softmax(cs, axis=-1)
        u = expert_u[eidx]
        v = expert_v[eidx]
        act = jax.nn.gelu(jnp.einsum('td,thkd->thk', xb, u).astype(jnp.float32), approximate=False)
        return jnp.einsum('thk,thkd->td', (g * act).astype(v.dtype), v)

    return lax.map(one, blocks).reshape(T, D)


def setup_inputs(seed: int = 0) -> dict:
    key = jax.random.key(seed)
    ks = jax.random.split(key, 22)
    f32 = jnp.float32

    def nrm(k, shape, scale):
        return jax.random.normal(k, shape, f32) * scale

    x = nrm(ks[0], (BATCH, SEQ, D_MODEL), 1.0)
    meta_tokens = nrm(ks[1], (N_META, D_MODEL), 1.0)
    ln_mix = 1.0 + nrm(ks[2], (DEPTH, D_MODEL), 0.02)
    w_in = nrm(ks[3], (DEPTH, D_MODEL, PROJ_TOTAL), D_MODEL ** -0.5)
    ssd_conv_w = nrm(ks[4], (DEPTH, SSD_CONV_DIM, SSD_CONV_W), SSD_CONV_W ** -0.5)
    ssd_conv_b = nrm(ks[5], (DEPTH, SSD_CONV_DIM), 0.01)
    dt0 = jnp.exp(jax.random.uniform(ks[6], (DEPTH, SSD_N_HEADS), f32, math.log(1e-3), math.log(1e-1)))
    ssd_dt_bias = dt0 + jnp.log(-jnp.expm1(-dt0))
    ssd_a_log = jnp.log(jax.random.uniform(ks[7], (DEPTH, SSD_N_HEADS), f32, 1.0, 16.0))
    ssd_d = 1.0 + nrm(ks[8], (DEPTH, SSD_N_HEADS), 0.02)
    ssd_norm = 1.0 + nrm(ks[9], (DEPTH, SSD_D_INNER), 0.02)
    ssd_w_out = nrm(ks[10], (DEPTH, SSD_D_INNER, D_MODEL), SSD_D_INNER ** -0.5)
    sc_conv_w = nrm(ks[11], (DEPTH, SC_WIDTH, SC_CONV_W), SC_CONV_W ** -0.5)
    sc_w_out = nrm(ks[12], (DEPTH, SC_WIDTH, D_MODEL), SC_WIDTH ** -0.5)
    w_o = nrm(ks[13], (DEPTH, D_MODEL, D_MODEL), D_MODEL ** -0.5)
    ln_ffn = 1.0 + nrm(ks[14], (DEPTH, D_MODEL), 0.02)
    peer_w_q = nrm(ks[15], (DEPTH, D_MODEL, PEER_HEADS * PEER_D_KEY), D_MODEL ** -0.5)
    peer_sub_keys = nrm(ks[16], (DEPTH, PEER_HEADS, 2, PEER_N_KEYS, PEER_HALF), PEER_HALF ** -0.5)
    peer_u = nrm(ks[17], (DEPTH, PEER_N_EXPERTS, D_MODEL), D_MODEL ** -0.5)
    peer_v = nrm(ks[18], (DEPTH, PEER_N_EXPERTS, D_MODEL), PEER_HEADS ** -0.5)
    ln_final = 1.0 + nrm(ks[19], (D_MODEL,), 0.02)
    return {"x": x, "meta_tokens": meta_tokens, "ln_mix": ln_mix, "w_in": w_in,
            "ssd_conv_w": ssd_conv_w, "ssd_conv_b": ssd_conv_b, "ssd_dt_bias": ssd_dt_bias,
            "ssd_a_log": ssd_a_log, "ssd_d": ssd_d, "ssd_norm": ssd_norm, "ssd_w_out": ssd_w_out,
            "sc_conv_w": sc_conv_w, "sc_w_out": sc_w_out, "w_o": w_o, "ln_ffn": ln_ffn,
            "peer_w_q": peer_w_q, "peer_sub_keys": peer_sub_keys, "peer_u": peer_u,
            "peer_v": peer_v, "ln_final": ln_final}


def reference(x, meta_tokens, ln_mix, w_in, ssd_conv_w, ssd_conv_b, ssd_dt_bias, ssd_a_log, ssd_d,
              ssd_norm, ssd_w_out, sc_conv_w, sc_w_out, w_o, ln_ffn, peer_w_q, peer_sub_keys,
              peer_u, peer_v, ln_final):
    b = x.shape[0]
    pad = jnp.zeros((b, N_PAD, D_MODEL), x.dtype)
    meta = jnp.broadcast_to(meta_tokens[None].astype(x.dtype), (b, N_META, D_MODEL))
    h = jnp.concatenate([pad, meta, x], axis=1)
    lp = h.shape[1]
    valid = (jnp.arange(lp) >= N_PAD)[None, :, None].astype(x.dtype)
    split_idx = [int(c) for c in np.cumsum(PROJ_SIZES)[:-1]]

    for l in range(DEPTH):
        u = rmsnorm(h, ln_mix[l]) * valid
        proj = u @ w_in[l]
        z, xs, bs, cs, dt, sb, sc, sx, gates = jnp.split(proj, split_idx, axis=-1)
        y_ssd = ssd_branch(z, xs, bs, cs, dt, valid, ssd_conv_w[l], ssd_conv_b[l], ssd_dt_bias[l],
                           ssd_a_log[l], ssd_d[l], ssd_norm[l], ssd_w_out[l])
        y_sc = shortconv_branch(sb, sc, sx, sc_conv_w[l], sc_w_out[l])
        g = jax.nn.sigmoid(gates.astype(jnp.float32))
        merged = (g[..., :D_MODEL] * y_ssd + g[..., D_MODEL:] * y_sc).astype(x.dtype)
        h = h + merged @ w_o[l]
        u = rmsnorm(h, ln_ffn[l]) * valid
        ff = peer_ffn(u.reshape(-1, D_MODEL), peer_w_q[l], peer_sub_keys[l], peer_u[l], peer_v[l])
        h = h + ff.reshape(h.shape).astype(h.dtype)

    out = rmsnorm(h[:, N_PAD + N_META:], ln_final)
    return out
```

```python
import functools

import jax
import jax.numpy as jnp
from jax import lax
from jax.experimental import pallas as pl
from jax.experimental.pallas import tpu as pltpu

F32 = jnp.float32
BF16 = jnp.bfloat16

EPS = 1e-6
CHUNK = 64
N_GROUPS = 8
HEADS_PER_GROUP = 4
N_HEADS = 32
HEAD_DIM = 64
D_STATE = 128
SSD_CONV_W = 4
SC_CONV_W = 3
PEER_HEADS = 8
PEER_N_KEYS = 128
PEER_TOPK = 16
LANES = 128
SUBLANES = 8
VMEM_LIMIT = 56 * 1024 * 1024

_HI = lax.Precision.HIGHEST
_NT = (((1,), (1,)), ((), ()))
_TN = (((0,), (0,)), ((), ()))


def _cparams(sem):
    return pltpu.CompilerParams(dimension_semantics=sem, vmem_limit_bytes=VMEM_LIMIT)


def _const_spec(shape):
    nd = len(shape)
    return pl.BlockSpec(shape, lambda *_: (0,) * nd)


def _in_proj_kernel(x_ref, ln_ref, w_ref, wdt_ref, proj_ref, dt_ref, u_ref):
    @pl.when(pl.program_id(1) == 0)
    def _():
        x = x_ref[...]
        ms = jnp.mean(x * x, axis=-1, keepdims=True)
        ub = (x * lax.rsqrt(ms + EPS) * ln_ref[...]).astype(BF16)
        u_ref[...] = ub
        dt_ref[...] = jnp.dot(ub, wdt_ref[...], preferred_element_type=F32)

    proj_ref[...] = jnp.dot(u_ref[...], w_ref[...], preferred_element_type=F32)


def _in_proj(x2d, ln, w_main, w_dt, tm, tn):
    t, d = x2d.shape
    n = w_main.shape[1]
    return pl.pallas_call(
        _in_proj_kernel,
        grid=(t // tm, n // tn),
        in_specs=[
            pl.BlockSpec((tm, d), lambda i, j: (i, 0)),
            _const_spec((1, d)),
            pl.BlockSpec((d, tn), lambda i, j: (0, j)),
            _const_spec((d, LANES)),
        ],
        out_specs=[
            pl.BlockSpec((tm, tn), lambda i, j: (i, j)),
            pl.BlockSpec((tm, LANES), lambda i, j: (i, 0)),
        ],
        out_shape=[
            jax.ShapeDtypeStruct((t, n), F32),
            jax.ShapeDtypeStruct((t, LANES), F32),
        ],
        scratch_shapes=[pltpu.VMEM((tm, d), BF16)],
        compiler_params=_cparams(("arbitrary", "arbitrary")),
        name="in_proj",
    )(x2d, ln, w_main, w_dt)


D_INNER = N_HEADS * HEAD_DIM
BC_DIM = N_GROUPS * D_STATE
CONV_DIM = D_INNER + 2 * BC_DIM
GROUP_W = HEADS_PER_GROUP * HEAD_DIM


def _ssd_kernel(xbc_ref, dtraw_ref, state0_ref, tail0_ref, cw_ref, cb_ref, dtb_ref, alog_ref,
                dfull_ref, expand_ref, y_ref, state_out_ref, tail_out_ref, xbuf_ref, state_ref,
                *, n_masked_rows):
    c = pl.program_id(1)

    @pl.when(c == 0)
    def _():
        xbuf_ref[0:SUBLANES, :] = tail0_ref[...]
        state_ref[...] = state0_ref[...]

    xbuf_ref[SUBLANES:SUBLANES + CHUNK, :] = xbc_ref[...]
    conv = cb_ref[...] + cw_ref[SSD_CONV_W - 1:SSD_CONV_W, :] * xbuf_ref[SUBLANES:SUBLANES + CHUNK, :]
    for k in range(SSD_CONV_W - 1):
        off = SUBLANES - (SSD_CONV_W - 1) + k
        conv = conv + cw_ref[k:k + 1, :] * xbuf_ref[off:off + CHUNK, :]
    xbuf_ref[0:SUBLANES, :] = xbuf_ref[CHUNK:CHUNK + SUBLANES, :]
    act = conv * jax.nn.sigmoid(conv)
    if n_masked_rows:
        rows = lax.broadcasted_iota(jnp.int32, (CHUNK, 1), 0)
        act = jnp.where(rows >= n_masked_rows, act, 0.0)
    xs = act[:, :D_INNER]
    bm = act[:, D_INNER:D_INNER + BC_DIM]
    cm = act[:, D_INNER + BC_DIM:]

    dt = jax.nn.softplus(dtraw_ref[...] + dtb_ref[...])
    adt = dt * (-jnp.exp(alog_ref[...]))
    li = lax.broadcasted_iota(jnp.int32, (CHUNK, CHUNK), 0)
    si = lax.broadcasted_iota(jnp.int32, (CHUNK, CHUNK), 1)
    causal = li >= si
    acum = jnp.dot(causal.astype(F32), adt, precision=_HI, preferred_element_type=F32)
    tot = acum[CHUNK - 1:CHUNK, :]
    expand = expand_ref[...]
    dt_full = jnp.dot(dt, expand, precision=_HI, preferred_element_type=F32)
    ea_full = jnp.dot(jnp.exp(acum), expand, precision=_HI, preferred_element_type=F32)
    te_full = jnp.dot(jnp.exp(tot - acum), expand, precision=_HI, preferred_element_type=F32)
    etot_full = jnp.dot(jnp.broadcast_to(jnp.exp(tot), (SUBLANES, LANES)), expand,
                        precision=_HI, preferred_element_type=F32)[0:1, :]
    acum_t = jnp.concatenate([acum, jnp.zeros_like(acum)], axis=0).T

    xdt = xs * dt_full
    xw = xdt * te_full
    y_parts = []
    for g in range(N_GROUPS):
        bg = bm[:, g * D_STATE:(g + 1) * D_STATE]
        cg = cm[:, g * D_STATE:(g + 1) * D_STATE]
        gl = slice(g * GROUP_W, (g + 1) * GROUP_W)
        cb = lax.dot_general(cg, bg, _NT, preferred_element_type=F32)
        sg = state_ref[g]
        y_off = jnp.dot(cg, sg, preferred_element_type=F32) * ea_full[:, gl]
        diag = []
        for r in range(HEADS_PER_GROUP):
            h = g * HEADS_PER_GROUP + r
            seg = acum[:, h:h + 1] - acum_t[h:h + 1, :CHUNK]
            m = cb * jnp.exp(jnp.where(causal, seg, -jnp.inf))
            diag.append(jnp.dot(m, xdt[:, h * HEAD_DIM:(h + 1) * HEAD_DIM],
                                preferred_element_type=F32))
        y_parts.append(jnp.concatenate(diag, axis=1) + y_off)
        state_ref[g] = sg * etot_full[:, gl] + lax.dot_general(
            bg, xw[:, gl], _TN, preferred_element_type=F32)
    y_ref[...] = jnp.concatenate(y_parts, axis=1) + xs * dfull_ref[...]

    @pl.when(c == pl.num_programs(1) - 1)
    def _():
        state_out_ref[...] = state_ref[...]
        tail_out_ref[...] = xbuf_ref[0:SUBLANES, :]


def _ssd_scan(proj, dtraw, state0, tail0, cw, cb, dtb, alog, dfull, expand, n_batch, n_chunks,
              n_masked_rows):
    t = n_batch * n_chunks * CHUNK
    kern = functools.partial(_ssd_kernel, n_masked_rows=n_masked_rows)
    return pl.pallas_call(
        kern,
        grid=(n_batch, n_chunks),
        in_specs=[
            pl.BlockSpec((CHUNK, CONV_DIM), lambda b, c: (b * n_chunks + c, 0)),
            pl.BlockSpec((CHUNK, LANES), lambda b, c: (b * n_chunks + c, 0)),
            _const_spec((N_GROUPS, D_STATE, GROUP_W)),
            _const_spec((SUBLANES, CONV_DIM)),
            _const_spec((SUBLANES, CONV_DIM)),
            _const_spec((1, CONV_DIM)),
            _const_spec((1, LANES)),
            _const_spec((1, LANES)),
            _const_spec((1, D_INNER)),
            _const_spec((LANES, D_INNER)),
        ],
        out_specs=[
            pl.BlockSpec((CHUNK, D_INNER), lambda b, c: (b * n_chunks + c, 0)),
            pl.BlockSpec((None, N_GROUPS, D_STATE, GROUP_W), lambda b, c: (b, 0, 0, 0)),
            pl.BlockSpec((None, SUBLANES, CONV_DIM), lambda b, c: (b, 0, 0)),
        ],
        out_shape=[
            jax.ShapeDtypeStruct((t, D_INNER), F32),
            jax.ShapeDtypeStruct((n_batch, N_GROUPS, D_STATE, GROUP_W), F32),
            jax.ShapeDtypeStruct((n_batch, SUBLANES, CONV_DIM), F32),
        ],
        scratch_shapes=[
            pltpu.VMEM((CHUNK + SUBLANES, CONV_DIM), F32),
            pltpu.VMEM((N_GROUPS, D_STATE, GROUP_W), F32),
        ],
        compiler_params=_cparams(("arbitrary", "arbitrary")),
        name="ssd_scan",
    )(proj, dtraw, state0, tail0, cw, cb, dtb, alog, dfull, expand)


D_MODEL = 1024
Z_COL = CONV_DIM // D_INNER
G_COL = Z_COL + 1
S_COL = (CONV_DIM + 2 * D_INNER) // D_MODEL
N_SCORE = 2 * PEER_HEADS


def _post_mix_kernel(y_ref, z_ref, gates_ref, sb_ref, sc_ref, sx_ref, hsc_ref, hsx_ref,
                     h0sc_ref, h0sx_ref, x_ref, ng_ref, wout_ref, scw_ref, scwout_ref, wo_ref,
                     lnf_ref, wq_ref, keys_ref, h2_ref, u2_ref, st_ref, vbuf_ref):
    tm = y_ref.shape[0]
    z = z_ref[...]
    gated = y_ref[...] * (z * jax.nn.sigmoid(z))
    gw = D_INNER // N_GROUPS
    parts = []
    for g in range(N_GROUPS):
        sl = gated[:, g * gw:(g + 1) * gw]
        ms = jnp.mean(sl * sl, axis=-1, keepdims=True)
        parts.append(sl * lax.rsqrt(ms + EPS))
    yn = jnp.concatenate(parts, axis=1) * ng_ref[...]
    y_ssd = jnp.dot(yn.astype(BF16), wout_ref[...], preferred_element_type=F32)

    first = pl.program_id(1) == 0
    halo = jnp.where(first, h0sc_ref[...] * h0sx_ref[...], hsc_ref[...] * hsx_ref[...])
    vbuf_ref[0:SUBLANES, :] = halo
    vbuf_ref[SUBLANES:SUBLANES + tm, :] = sc_ref[...] * sx_ref[...]
    cv = scw_ref[SC_CONV_W - 1:SC_CONV_W, :] * vbuf_ref[SUBLANES:SUBLANES + tm, :]
    for k in range(SC_CONV_W - 1):
        off = SUBLANES - (SC_CONV_W - 1) + k
        cv = cv + scw_ref[k:k + 1, :] * vbuf_ref[off:off + tm, :]
    y_sc = jnp.dot((sb_ref[...] * cv).astype(BF16), scwout_ref[...], preferred_element_type=F32)

    gs = jax.nn.sigmoid(gates_ref[...])
    merged = gs[:, :D_MODEL] * y_ssd + gs[:, D_MODEL:] * y_sc
    h2 = x_ref[...] + jnp.dot(merged.astype(BF16), wo_ref[...], preferred_element_type=F32)
    h2_ref[...] = h2
    ms = jnp.mean(h2 * h2, axis=-1, keepdims=True)
    u2 = (h2 * lax.rsqrt(ms + EPS) * lnf_ref[...]).astype(BF16)
    u2_ref[...] = u2
    q = jnp.dot(u2, wq_ref[...], preferred_element_type=F32).astype(BF16)
    for hj in range(N_SCORE):
        st_ref[hj] = lax.dot_general(keys_ref[hj], q[:, hj * LANES:(hj + 1) * LANES], _NT,
                                     preferred_element_type=F32)


def _post_mix(y, proj, proj0, x2d, ng, wout, scw, scwout, wo, lnf, wq, keys, n_batch, seq, tm):
    t = n_batch * seq
    tiles = seq // tm
    rows8 = tm // SUBLANES

    def row(b, i):
        return b * tiles + i

    def halo_row(b, i):
        return jnp.maximum((b * tiles + i) * rows8 - 1, 0)

    last0 = CHUNK // SUBLANES - 1
    return pl.pallas_call(
        _post_mix_kernel,
        grid=(n_batch, tiles),
        in_specs=[
            pl.BlockSpec((tm, D_INNER), lambda b, i: (row(b, i), 0)),
            pl.BlockSpec((tm, D_INNER), lambda b, i: (row(b, i), Z_COL)),
            pl.BlockSpec((tm, D_INNER), lambda b, i: (row(b, i), G_COL)),
            pl.BlockSpec((tm, D_MODEL), lambda b, i: (row(b, i), S_COL)),
            pl.BlockSpec((tm, D_MODEL), lambda b, i: (row(b, i), S_COL + 1)),
            pl.BlockSpec((tm, D_MODEL), lambda b, i: (row(b, i), S_COL + 2)),
            pl.BlockSpec((SUBLANES, D_MODEL), lambda b, i: (halo_row(b, i), S_COL + 1)),
            pl.BlockSpec((SUBLANES, D_MODEL), lambda b, i: (halo_row(b, i), S_COL + 2)),
            pl.BlockSpec((SUBLANES, D_MODEL), lambda b, i: (last0, S_COL + 1)),
            pl.BlockSpec((SUBLANES, D_MODEL), lambda b, i: (last0, S_COL + 2)),
            pl.BlockSpec((tm, D_MODEL), lambda b, i: (row(b, i), 0)),
            _const_spec((1, D_INNER)),
            _const_spec((D_INNER, D_MODEL)),
            _const_spec((SUBLANES, D_MODEL)),
            _const_spec((D_MODEL, D_MODEL)),
            _const_spec((D_MODEL, D_MODEL)),
            _const_spec((1, D_MODEL)),
            _const_spec((D_MODEL, N_SCORE * LANES)),
            _const_spec((N_SCORE, PEER_N_KEYS, LANES)),
        ],
        out_specs=[
            pl.BlockSpec((tm, D_MODEL), lambda b, i: (row(b, i), 0)),
            pl.BlockSpec((tm, D_MODEL), lambda b, i: (row(b, i), 0)),
            pl.BlockSpec((N_SCORE, PEER_N_KEYS, tm), lambda b, i: (0, 0, row(b, i))),
        ],
        out_shape=[
            jax.ShapeDtypeStruct((t, D_MODEL), F32),
            jax.ShapeDtypeStruct((t, D_MODEL), BF16),
            jax.ShapeDtypeStruct((N_SCORE, PEER_N_KEYS, t), F32),
        ],
        scratch_shapes=[pltpu.VMEM((tm + SUBLANES, D_MODEL), F32)],
        compiler_params=_cparams(("arbitrary", "arbitrary")),
        name="post_mix",
    )(y, proj, proj, proj, proj, proj, proj, proj, proj0, proj0, x2d, ng, wout, scw, scwout, wo,
      lnf, wq, keys)


def _top_values(s, k):
    out = []
    for _ in range(k):
        m = jnp.max(s, axis=0, keepdims=True)
        out.append(m)
        s = jnp.where(s == m, -jnp.inf, s)
    return out


def _peer_sel_kernel(st_ref, c_ref, tau_ref):
    n_lane_blocks = st_ref.shape[2] // LANES

    def body(idx, carry):
        h = idx // n_lane_blocks
        lanes = pl.ds(pl.multiple_of((idx % n_lane_blocks) * LANES, LANES), LANES)
        sv1 = _top_values(st_ref[2 * h, :, lanes], PEER_TOPK)
        sv2 = jnp.concatenate(_top_values(st_ref[2 * h + 1, :, lanes], PEER_TOPK), axis=0)
        cand = jnp.concatenate([v + sv2 for v in sv1], axis=0)
        cs = _top_values(cand, PEER_TOPK)
        zsum = jnp.ones_like(cs[0])
        for v in cs[1:]:
            zsum = zsum + jnp.exp(v - cs[0])
        c_ref[h, :, lanes] = cs[0] + jnp.log(zsum)
        tau_ref[h, :, lanes] = cs[PEER_TOPK - 1]
        return carry

    lax.fori_loop(0, PEER_HEADS * n_lane_blocks, body, 0)


def _peer_sel(st, tl):
    t = st.shape[2]
    return pl.pallas_call(
        _peer_sel_kernel,
        grid=(t // tl,),
        in_specs=[pl.BlockSpec((N_SCORE, PEER_N_KEYS, tl), lambda i: (0, 0, i))],
        out_specs=[pl.BlockSpec((PEER_HEADS, 1, tl), lambda i: (0, 0, i))] * 2,
        out_shape=[jax.ShapeDtypeStruct((PEER_HEADS, 1, t), F32)] * 2,
        compiler_params=_cparams(("arbitrary",)),
        name="peer_sel",
    )(st)


_SQRT_HALF = 0.7071067811865476


def _peer_ffn_kernel(u2_ref, eu_ref, evt_ref, st_ref, c_ref, tau_ref, h2_ref, lnz_ref, out_ref,
                     ht_ref, p_ref, acc_ref):
    j = pl.program_id(1)
    et, tt = ht_ref.shape
    a_blocks = et // PEER_N_KEYS
    lane_blocks = tt // LANES

    @pl.when(j == 0)
    def _():
        acc_ref[...] = jnp.zeros_like(acc_ref)

    ht_ref[...] = lax.dot_general(eu_ref[...], u2_ref[...], _NT, preferred_element_type=F32)

    a_rows = pl.ds(pl.multiple_of(j * a_blocks, a_blocks), a_blocks)

    def body(lb, carry):
        lanes = pl.ds(pl.multiple_of(lb * LANES, LANES), LANES)
        s1 = [st_ref[2 * h, a_rows, lanes] for h in range(PEER_HEADS)]
        for ai in range(a_blocks):
            rows = slice(ai * PEER_N_KEYS, (ai + 1) * PEER_N_KEYS)
            w = jnp.zeros((PEER_N_KEYS, LANES), F32)
            for h in range(PEER_HEADS):
                s = s1[h][ai:ai + 1, :] + st_ref[2 * h + 1, :, lanes]
                e = jnp.exp(s - c_ref[h, :, lanes])
                w = w + jnp.where(s >= tau_ref[h, :, lanes], e, 0.0)
            hv = ht_ref[rows, lanes]
            act = 0.5 * hv * (1.0 + lax.erf(hv * _SQRT_HALF))
            p_ref[rows, lanes] = (w * act).astype(BF16)
        return carry

    lax.fori_loop(0, lane_blocks, body, 0)
    acc_ref[...] += jnp.dot(evt_ref[...], p_ref[...], preferred_element_type=F32)

    @pl.when(j == pl.num_programs(1) - 1)
    def _():
        h3 = h2_ref[...] + acc_ref[...].T
        ms = jnp.mean(h3 * h3, axis=-1, keepdims=True)
        out_ref[...] = h3 * lax.rsqrt(ms + EPS) * lnz_ref[...]


def _peer_ffn(u2, eu, evt, st, cmat, tau, h2, lnz, tt, et):
    t = u2.shape[0]
    ne = eu.shape[0]
    assert et % (SUBLANES * PEER_N_KEYS) == 0, "first-half key rows are loaded in aligned groups"
    return pl.pallas_call(
        _peer_ffn_kernel,
        grid=(t // tt, ne // et),
        in_specs=[
            pl.BlockSpec((tt, D_MODEL), lambda i, j: (i, 0)),
            pl.BlockSpec((et, D_MODEL), lambda i, j: (j, 0)),
            pl.BlockSpec((D_MODEL, et), lambda i, j: (0, j)),
            pl.BlockSpec((N_SCORE, PEER_N_KEYS, tt), lambda i, j: (0, 0, i)),
            pl.BlockSpec((PEER_HEADS, 1, tt), lambda i, j: (0, 0, i)),
            pl.BlockSpec((PEER_HEADS, 1, tt), lambda i, j: (0, 0, i)),
            pl.BlockSpec((tt, D_MODEL), lambda i, j: (i, 0)),
            _const_spec((1, D_MODEL)),
        ],
        out_specs=pl.BlockSpec((tt, D_MODEL), lambda i, j: (i, 0)),
        out_shape=jax.ShapeDtypeStruct((t, D_MODEL), F32),
        scratch_shapes=[
            pltpu.VMEM((et, tt), F32),
            pltpu.VMEM((et, tt), BF16),
            pltpu.VMEM((D_MODEL, tt), F32),
        ],
        compiler_params=_cparams(("arbitrary", "arbitrary")),
        name="peer_ffn",
    )(u2, eu, evt, st, cmat, tau, h2, lnz)


def _largest_tile(n, cap):
    t = min(n, cap)
    while n % t:
        t //= 2
    return t


def _pad_rows(w, rows):
    return jnp.pad(w, ((0, rows - w.shape[0]), (0, 0)))


def _pad_lanes(v, lanes):
    return jnp.pad(v, ((0, 0), (0, lanes - v.shape[1])))


def kernel(x, meta_tokens, ln_mix, w_in, ssd_conv_w, ssd_conv_b, ssd_dt_bias, ssd_a_log, ssd_d,
           ssd_norm, ssd_w_out, sc_conv_w, sc_w_out, w_o, ln_ffn, peer_w_q, peer_sub_keys,
           peer_u, peer_v, ln_final):
    n_batch, seq, d = x.shape
    n_meta = meta_tokens.shape[0]
    assert d == D_MODEL and seq % CHUNK == 0 and n_meta <= CHUNK
    assert w_in.shape[0] == 1, "single-layer block"
    n_chunks = seq // CHUNK
    t = n_batch * seq

    w = w_in[0]
    o_z, o_x, o_dt = 0, D_INNER, 2 * D_INNER + 2 * BC_DIM
    o_s = o_dt + N_HEADS
    o_g = o_s + 3 * D_MODEL
    w_main = jnp.concatenate([w[:, o_x:o_dt], w[:, o_z:o_x], w[:, o_g:], w[:, o_s:o_g]],
                             axis=1).astype(BF16)
    w_dt = _pad_lanes(w[:, o_dt:o_s], LANES).astype(BF16)
    ln1 = ln_mix[0][None, :]
    cw = _pad_rows(ssd_conv_w[0].T, SUBLANES)
    cb = ssd_conv_b[0][None, :]
    dtb = _pad_lanes(ssd_dt_bias[0][None, :], LANES)
    alog = _pad_lanes(ssd_a_log[0][None, :], LANES)
    dfull = jnp.repeat(ssd_d[0], HEAD_DIM)[None, :]
    expand = (lax.broadcasted_iota(jnp.int32, (LANES, D_INNER), 1) // HEAD_DIM
              == lax.broadcasted_iota(jnp.int32, (LANES, D_INNER), 0)).astype(F32)
    scw = _pad_rows(sc_conv_w[0].T, SUBLANES)
    keys = peer_sub_keys[0].reshape(N_SCORE, PEER_N_KEYS, LANES).astype(BF16)
    eu = peer_u[0].astype(BF16)
    evt = peer_v[0].T.astype(BF16)

    x0 = jnp.concatenate([jnp.zeros((CHUNK - n_meta, d), x.dtype), meta_tokens.astype(x.dtype)], axis=0)
    tn = _largest_tile(w_main.shape[1], 1024)
    proj0, dt0 = _in_proj(x0, ln1, w_main, w_dt, CHUNK, tn)
    zero_state = jnp.zeros((N_GROUPS, D_STATE, GROUP_W), F32)
    zero_tail = jnp.zeros((SUBLANES, CONV_DIM), F32)
    _, state0, tail0 = _ssd_scan(proj0, dt0, zero_state, zero_tail, cw, cb, dtb, alog, dfull,
                                 expand, 1, 1, CHUNK - n_meta)

    x2d = x.reshape(t, d)
    proj, dtraw = _in_proj(x2d, ln1, w_main, w_dt, _largest_tile(t, 512), tn)
    y, _, _ = _ssd_scan(proj, dtraw, state0[0], tail0[0], cw, cb, dtb, alog, dfull, expand,
                        n_batch, n_chunks, 0)
    h2, u2, st = _post_mix(y, proj, proj0, x2d, ssd_norm[0][None, :], ssd_w_out[0].astype(BF16),
                           scw, sc_w_out[0].astype(BF16), w_o[0].astype(BF16),
                           ln_ffn[0][None, :], peer_w_q[0].astype(BF16), keys,
                           n_batch, seq, _largest_tile(seq, 256))
    cmat, tau = _peer_sel(st, _largest_tile(t, 1024))
    out = _peer_ffn(u2, eu, evt, st, cmat, tau, h2, ln_final[None, :],
                    _largest_tile(t, 512), SUBLANES * PEER_N_KEYS)
    return out.reshape(n_batch, seq, d)
```
